```python
import math
import jax, jax.numpy as jnp
from jax import lax
import numpy as np

D_MODEL = 1024
BATCH = 4
SEQ = 4096
DEPTH = 1

HEAD_DIM = 64
N_HEADS_DIL = 8
N_HEADS_SB = 8
D_DIL = N_HEADS_DIL * HEAD_DIM
D_SB = N_HEADS_SB * HEAD_DIM
D_MIX = D_DIL + D_SB
DIL_PATTERNS = ((128, 1), (512, 4), (2048, 16))
Q_BLOCK = 128
ROPE_THETA = 500000.0
ROPE_DIM = HEAD_DIM // 4
N_MEM = 256
MEM_HEADS = 4
MEM_HEAD_DIM = 128
D_MEM_ATTN = MEM_HEADS * MEM_HEAD_DIM
PEER_HEADS = 8
PEER_NKEYS = 128
PEER_N_EXPERTS = PEER_NKEYS * PEER_NKEYS
PEER_DKEY = 256
PEER_TOPK = 16
PEER_TOKEN_BLOCK = 128
EPS = 1e-6
POS_OFFSET_MAX = 1024

kernel_name = "hybrid_dilated_stickbreaking_peer_block"


def rmsnorm(x, g):
    xf = x.astype(jnp.float32)
    y = xf * lax.rsqrt(jnp.mean(xf * xf, axis=-1, keepdims=True) + EPS)
    return (y * g.astype(jnp.float32)).astype(x.dtype)


def partial_rope(x, positions):
    inv_freq = ROPE_THETA ** (-jnp.arange(0, ROPE_DIM, 2, dtype=jnp.float32) / ROPE_DIM)
    ang = positions.astype(jnp.float32)[:, None, :, None] * inv_freq
    cos, sin = jnp.cos(ang), jnp.sin(ang)
    xf = x.astype(jnp.float32)
    x1 = xf[..., :ROPE_DIM // 2]
    x2 = xf[..., ROPE_DIM // 2:ROPE_DIM]
    out = jnp.concatenate([x1 * cos - x2 * sin, x2 * cos + x1 * sin, xf[..., ROPE_DIM:]], axis=-1)
    return out.astype(x.dtype)


def split_heads(t, n_heads, head_dim):
    b, s, _ = t.shape
    return t.reshape(b, s, n_heads, head_dim).transpose(0, 2, 1, 3)


def merge_heads(t):
    b, h, s, d = t.shape
    return t.transpose(0, 2, 1, 3).reshape(b, s, h * d)


def dilated_window_attention(q, k, v):
    b, h, s, dh = q.shape
    scale = dh ** -0.5
    n_blocks = s // Q_BLOCK

    def block(i):
        start = i * Q_BLOCK
        qb = lax.dynamic_slice_in_dim(q, start, Q_BLOCK, axis=2)
        t = start + jnp.arange(Q_BLOCK)
        outs, lses = [], []
        for window, dilation in DIL_PATTERNS:
            n_k = window // dilation + 1
            idx = t[:, None] - dilation * jnp.arange(n_k)[None, :]
            valid = idx >= 0
            idx = jnp.maximum(idx, 0)
            kg = jnp.take(k, idx, axis=2)
            vg = jnp.take(v, idx, axis=2)
            sc = jnp.einsum('bhqd,bhqjd->bhqj', qb, kg).astype(jnp.float32) * scale
            sc = jnp.where(valid, sc, -jnp.inf)
            m = jnp.max(sc, axis=-1, keepdims=True)
            p = jnp.exp(sc - m)
            den = jnp.sum(p, axis=-1, keepdims=True)
            o = jnp.einsum('bhqj,bhqjd->bhqd', p, vg.astype(jnp.float32)) / den
            outs.append(o)
            lses.append(m + jnp.log(den))
        lse = jnp.concatenate(lses, axis=-1)
        w = jax.nn.softmax(lse, axis=-1)
        o = jnp.einsum('bhqp,bhqpd->bhqd', w, jnp.stack(outs, axis=-2))
        return o.astype(q.dtype)

    out = lax.map(block, jnp.arange(n_blocks))
    return jnp.moveaxis(out, 0, 2).reshape(b, h, s, dh)


def stick_breaking_attention(q, k, v):
    b, h, s, dh = q.shape
    scale = dh ** -0.5
    n_blocks = s // Q_BLOCK
    key_pos = jnp.arange(s)
    vf = v.astype(jnp.float32)

    def block(i):
        start = i * Q_BLOCK
        qb = lax.dynamic_slice_in_dim(q, start, Q_BLOCK, axis=2)
        t = start + jnp.arange(Q_BLOCK)
        z = jnp.einsum('bhqd,bhkd->bhqk', qb, k).astype(jnp.float32) * scale
        causal = key_pos[None, :] < t[:, None]
        log_1m = jnp.where(causal, jax.nn.log_sigmoid(-z), 0.0)
        after = lax.cumsum(log_1m, axis=3, reverse=True) - log_1m
        a = jnp.where(causal, jnp.exp(jax.nn.log_sigmoid(z) + after), 0.0)
        o = jnp.einsum('bhqk,bhkd->bhqd', a, vf)
        return o.astype(q.dtype)

    out = lax.map(block, jnp.arange(n_blocks))
    return jnp.moveaxis(out, 0, 2).reshape(b, h, s, dh)


def memory_cross_attention(h_n, mem_n, w_q, w_kv, g_q, g_k, w_o):
    q = split_heads(h_n @ w_q, MEM_HEADS, MEM_HEAD_DIM)
    kv = mem_n @ w_kv
    k = split_heads(kv[..., :D_MEM_ATTN], MEM_HEADS, MEM_HEAD_DIM)
    v = split_heads(kv[..., D_MEM_ATTN:], MEM_HEADS, MEM_HEAD_DIM)
    q = rmsnorm(q, g_q)
    k = rmsnorm(k, g_k)
    sc = jnp.einsum('bhsd,bhmd->bhsm', q, k).astype(jnp.float32) * (MEM_HEAD_DIM ** -0.5)
    p = jax.nn.softmax(sc, axis=-1)
    o = jnp.einsum('bhsm,bhmd->bhsd', p, v.astype(jnp.float32)).astype(h_n.dtype)
    return merge_heads(o) @ w_o


def peer_ffn(x_n, w_q, sub_keys, u, v):
    b, s, d = x_n.shape
    n_blocks = (b * s) // PEER_TOKEN_BLOCK
    xb = x_n.reshape(n_blocks, PEER_TOKEN_BLOCK, d)

    def block(xc):
        q = (xc @ w_q).reshape(PEER_TOKEN_BLOCK, PEER_HEADS, 2, PEER_DKEY // 2)
        sc = jnp.einsum('thpc,hpnc->thpn', q, sub_keys).astype(jnp.float32)
        top_s, top_i = lax.top_k(sc, PEER_TOPK)
        cand_s = top_s[:, :, 0, :, None] + top_s[:, :, 1, None, :]
        cand_i = top_i[:, :, 0, :, None] * PEER_NKEYS + top_i[:, :, 1, None, :]
        cand_s = cand_s.reshape(PEER_TOKEN_BLOCK, PEER_HEADS, PEER_TOPK * PEER_TOPK)
        cand_i = cand_i.reshape(PEER_TOKEN_BLOCK, PEER_HEADS, PEER_TOPK * PEER_TOPK)
        best_s, best_pos = lax.top_k(cand_s, PEER_TOPK)
        experts = jnp.take_along_axis(cand_i, best_pos, axis=-1)
        g = jax.nn.softmax(best_s, axis=-1)
        ue = jnp.take(u, experts, axis=0)
        ve = jnp.take(v, experts, axis=0)
        act = jax.nn.gelu(jnp.einsum('td,thkd->thk', xc, ue).astype(jnp.float32), approximate=False)
        y = jnp.einsum('thk,thkd->td', (g * act).astype(ve.dtype), ve)
        return y

    y = lax.map(block, xb)
    return y.reshape(b, s, d)


def setup_inputs(seed: int = 0) -> dict:
    key = jax.random.key(seed)
    ks = jax.random.split(key, 24)
    f32 = jnp.float32

    def nrm(k, shape, scale):
        return jax.random.normal(k, shape, f32) * scale

    def gain(k, shape):
        return 1.0 + 0.02 * jax.random.normal(k, shape, f32)

    x = jax.random.normal(ks[0], (BATCH, SEQ, D_MODEL), f32)
    mem = jax.random.normal(ks[1], (BATCH, N_MEM, D_MODEL), f32)
    offset = jax.random.randint(ks[2], (BATCH, 1), 0, POS_OFFSET_MAX, dtype=jnp.int32)
    positions = (offset + jnp.arange(SEQ, dtype=jnp.int32)[None, :]).astype(jnp.int32)
    return {
        "x": x,
        "mem": mem,
        "positions": positions,
        "norm_mix_g": gain(ks[3], (DEPTH, D_MODEL)),
        "w_in": nrm(ks[4], (DEPTH, D_MODEL, 3 * D_MIX), D_MODEL ** -0.5),
        "qnorm_g_dil": gain(ks[5], (DEPTH, HEAD_DIM)),
        "knorm_g_dil": gain(ks[6], (DEPTH, HEAD_DIM)),
        "out_norm_g_dil": gain(ks[7], (DEPTH, D_DIL)),
        "out_norm_g_sb": gain(ks[8], (DEPTH, D_SB)),
        "w_out": nrm(ks[9], (DEPTH, D_MIX, D_MODEL), D_MIX ** -0.5),
        "norm_mem_g": gain(ks[10], (DEPTH, D_MODEL)),
        "norm_memtok_g": gain(ks[11], (DEPTH, D_MODEL)),
        "w_q_mem": nrm(ks[12], (DEPTH, D_MODEL, D_MEM_ATTN), D_MODEL ** -0.5),
        "w_kv_mem": nrm(ks[13], (DEPTH, D_MODEL, 2 * D_MEM_ATTN), D_MODEL ** -0.5),
        "qnorm_g_mem": gain(ks[14], (DEPTH, MEM_HEAD_DIM)),
        "knorm_g_mem": gain(ks[15], (DEPTH, MEM_HEAD_DIM)),
        "w_o_mem": nrm(ks[16], (DEPTH, D_MEM_ATTN, D_MODEL), D_MEM_ATTN ** -0.5),
        "norm_ffn_g": gain(ks[17], (DEPTH, D_MODEL)),
        "w_q_peer": nrm(ks[18], (DEPTH, D_MODEL, PEER_HEADS * PEER_DKEY), D_MODEL ** -0.5),
        "peer_sub_keys": nrm(ks[19], (DEPTH, PEER_HEADS, 2, PEER_NKEYS, PEER_DKEY // 2), (PEER_DKEY // 2) ** -0.5),
        "peer_u": nrm(ks[20], (DEPTH, PEER_N_EXPERTS, D_MODEL), D_MODEL ** -0.5),
        "peer_v": nrm(ks[21], (DEPTH, PEER_N_EXPERTS, D_MODEL), 0.25),
    }


def reference(x, mem, positions, norm_mix_g, w_in, qnorm_g_dil, knorm_g_dil, out_norm_g_dil,
              out_norm_g_sb, w_out, norm_mem_g, norm_memtok_g, w_q_mem, w_kv_mem, qnorm_g_mem,
              knorm_g_mem, w_o_mem, norm_ffn_g, w_q_peer, peer_sub_keys, peer_u, peer_v):
    h = x
    for l in range(DEPTH):
        h_n = rmsnorm(h, norm_mix_g[l])
        proj = h_n @ w_in[l]
        q_a = proj[..., 0 * D_DIL:1 * D_DIL]
        k_a = proj[..., 1 * D_DIL:2 * D_DIL]
        v_a = proj[..., 2 * D_DIL:3 * D_DIL]
        off = 3 * D_DIL
        q_b = proj[..., off + 0 * D_SB:off + 1 * D_SB]
        k_b = proj[..., off + 1 * D_SB:off + 2 * D_SB]
        v_b = proj[..., off + 2 * D_SB:off + 3 * D_SB]

        qa = partial_rope(rmsnorm(split_heads(q_a, N_HEADS_DIL, HEAD_DIM), qnorm_g_dil[l]), positions)
        ka = partial_rope(rmsnorm(split_heads(k_a, N_HEADS_DIL, HEAD_DIM), knorm_g_dil[l]), positions)
        va = split_heads(v_a, N_HEADS_DIL, HEAD_DIM)
        o_a = merge_heads(dilated_window_attention(qa, ka, va))

        qb = split_heads(q_b, N_HEADS_SB, HEAD_DIM)
        kb = split_heads(k_b, N_HEADS_SB, HEAD_DIM)
        vb = split_heads(v_b, N_HEADS_SB, HEAD_DIM)
        o_b = merge_heads(stick_breaking_attention(qb, kb, vb))

        mixed = jnp.concatenate([rmsnorm(o_a, out_norm_g_dil[l]), rmsnorm(o_b, out_norm_g_sb[l])], axis=-1)
        h = h + mixed @ w_out[l]

        h_n = rmsnorm(h, norm_mem_g[l])
        mem_n = rmsnorm(mem, norm_memtok_g[l])
        h = h + memory_cross_attention(h_n, mem_n, w_q_mem[l], w_kv_mem[l], qnorm_g_mem[l],
                                       knorm_g_mem[l], w_o_mem[l])

        h_n = rmsnorm(h, norm_ffn_g[l])
        h = h + peer_ffn(h_n, w_q_peer[l], peer_sub_keys[l], peer_u[l], peer_v[l])
    return h
```

```python
import functools
import math

import jax
import jax.numpy as jnp
from jax import lax
from jax.experimental import pallas as pl
from jax.experimental.pallas import tpu as pltpu

F32 = jnp.float32
BF16 = jnp.bfloat16

LANES = 128
SUBLANES = 8
VMEM_LIMIT_BYTES = 56 * 1024 * 1024

EPS = 1e-6
HEAD_DIM = 64
HEADS_PER_BLOCK = LANES // HEAD_DIM
DIL_PATTERNS = ((128, 1), (512, 4), (2048, 16))
ATT_BLOCK = 128
ROPE_THETA = 500000.0
ROPE_DIM = HEAD_DIM // 4
MEM_HEADS = 4
MEM_HEAD_DIM = 128
PEER_HEADS = 8
PEER_NKEYS = 128
PEER_TOPK = 16
INV_SQRT2 = 1.0 / math.sqrt(2.0)

NT_DIMS = (((1,), (1,)), ((), ()))


def _rms(x, gain):
    return x * lax.rsqrt(jnp.mean(x * x, axis=-1, keepdims=True) + EPS) * gain


def _dot(a, b):
    return jnp.dot(a, b, preferred_element_type=F32)


def _dot_nt(a, b):
    return lax.dot_general(a, b, NT_DIMS, preferred_element_type=F32)


def _split_bf16(x):
    hi = x.astype(BF16)
    lo = (x - hi.astype(F32)).astype(BF16)
    return hi, lo


def _params(*semantics):
    return pltpu.CompilerParams(dimension_semantics=semantics,
                                vmem_limit_bytes=VMEM_LIMIT_BYTES)


def _in_proj_kernel(x_ref, pos_ref, g_ref, w_ref, qg_ref, kg_ref, invf_ref,
                    qa_ref, ka_ref, va_ref, qb_ref, kb_ref, vb_ref, *, d_grp):
    hb = _rms(x_ref[...], g_ref[...]).astype(BF16)

    def proj(c):
        return _dot(hb, w_ref[:, c * d_grp:(c + 1) * d_grp])

    ang = pos_ref[...].astype(F32) * invf_ref[...]
    cos = jnp.cos(ang)
    sin = jnp.sin(ang)
    lane = lax.broadcasted_iota(jnp.int32, (1, LANES), 1) % HEAD_DIM
    half = ROPE_DIM // 2
    sin_lo = jnp.where(lane < half, -sin, 0.0)
    sin_hi = jnp.where(lane < half, 0.0, sin)
    r = lax.broadcasted_iota(jnp.int32, (LANES, LANES), 0) // HEAD_DIM
    c = lax.broadcasted_iota(jnp.int32, (LANES, LANES), 1) // HEAD_DIM
    head_mean = jnp.where(r == c, 1.0 / HEAD_DIM, 0.0).astype(BF16)

    def qk_norm_rope(t, gain, scale):
        outs = []
        for j in range(d_grp // LANES):
            tc = t[:, j * LANES:(j + 1) * LANES]
            hi, lo = _split_bf16(tc * tc)
            ms = _dot(hi, head_mean) + _dot(lo, head_mean)
            y = tc * lax.rsqrt(ms + EPS) * gain
            y = (y * cos + pltpu.roll(y, LANES - half, 1) * sin_lo
                 + pltpu.roll(y, half, 1) * sin_hi)
            outs.append(y * scale)
        return jnp.concatenate(outs, axis=1)

    scale = HEAD_DIM ** -0.5
    qa_ref[...] = qk_norm_rope(proj(0), qg_ref[...], scale)
    ka_ref[...] = qk_norm_rope(proj(1), kg_ref[...], 1.0)
    va_ref[...] = proj(2)
    qb_ref[...] = (proj(3) * scale).astype(BF16)
    kb_ref[...] = proj(4).astype(BF16)
    vb_ref[...] = proj(5).astype(BF16)


def _in_proj(x2, pos2, g, w_in, qg, kg, invf, *, block_rows):
    n, d = x2.shape
    d_grp = w_in.shape[1] // 6
    row = lambda i: (i, 0)
    fixed = lambda i: (0, 0)
    out_f32 = jax.ShapeDtypeStruct((n, d_grp), F32)
    out_bf16 = jax.ShapeDtypeStruct((n, d_grp), BF16)
    return pl.pallas_call(
        functools.partial(_in_proj_kernel, d_grp=d_grp),
        grid=(n // block_rows,),
        in_specs=[
            pl.BlockSpec((block_rows, d), row),
            pl.BlockSpec((block_rows, 1), row),
            pl.BlockSpec((1, d), fixed),
            pl.BlockSpec(w_in.shape, fixed),
            pl.BlockSpec((1, LANES), fixed),
            pl.BlockSpec((1, LANES), fixed),
            pl.BlockSpec((1, LANES), fixed),
        ],
        out_specs=[pl.BlockSpec((block_rows, d_grp), row)] * 6,
        out_shape=[out_f32, out_f32, out_f32, out_bf16, out_bf16, out_bf16],
        compiler_params=_params("parallel"),
        name="in_proj",
    )(x2, pos2, g, w_in, qg, kg, invf)


def _dilated_kernel(q_ref, k_ref, v_ref, o_ref, m_scr, l_scr, acc_scr, *, seq):
    blk = ATT_BLOCK
    m_scr[...] = jnp.full(m_scr.shape, -jnp.inf, F32)
    l_scr[...] = jnp.zeros(l_scr.shape, F32)
    acc_scr[...] = jnp.zeros(acc_scr.shape, F32)

    lane = lax.broadcasted_iota(jnp.int32, (1, LANES), 1)
    head_masks = [(lane // HEAD_DIM) == h for h in range(HEADS_PER_BLOCK)]
    rel = (lax.broadcasted_iota(jnp.int32, (blk, 2 * blk), 0)
           - lax.broadcasted_iota(jnp.int32, (blk, 2 * blk), 1))
    bias_first = jnp.where(rel >= 0, 0.0, -jnp.inf).astype(F32)
    bias_rest = jnp.where((rel + blk >= 0) & (rel <= 0), 0.0, -jnp.inf).astype(F32)

    for p_idx, (window, dil) in enumerate(DIL_PATTERNS):
        assert window // dil == blk
        n_blk = seq // (dil * blk)
        assert n_blk >= 2
        last_pattern = p_idx == len(DIL_PATTERNS) - 1

        def rows(start, size, dil=dil):
            return pl.ds(start, size, stride=dil) if dil > 1 else pl.ds(start, size)

        def block(start_q, start_k, bias, last_pattern=last_pattern, rows=rows):
            q = q_ref[rows(start_q, blk), :]
            k = k_ref[rows(start_k, 2 * blk), :].astype(BF16)
            v = v_ref[rows(start_k, 2 * blk), :].astype(BF16)
            m_b = l_b = pv_b = None
            for hm in head_masks:
                qh = jnp.where(hm, q, 0.0).astype(BF16)
                s = _dot_nt(qh, k) + bias
                m_h = jnp.max(s, axis=1, keepdims=True)
                p = jnp.exp(s - m_h)
                l_h = jnp.sum(p, axis=1, keepdims=True)
                pv_h = _dot(p.astype(BF16), v)
                if m_b is None:
                    m_b, l_b, pv_b = m_h, l_h, pv_h
                else:
                    m_b = jnp.where(hm, m_h, m_b)
                    l_b = jnp.where(hm, l_h, l_b)
                    pv_b = jnp.where(hm, pv_h, pv_b)
            m_old = m_scr[rows(start_q, blk), :]
            l_old = l_scr[rows(start_q, blk), :]
            acc_old = acc_scr[rows(start_q, blk), :]
            m_new = jnp.maximum(m_old, m_b)
            a_old = jnp.exp(m_old - m_new)
            a_blk = jnp.exp(m_b - m_new)
            l_new = a_old * l_old + a_blk * l_b
            acc_new = a_old * acc_old + a_blk * pv_b
            if last_pattern:
                o_ref[rows(start_q, blk), :] = acc_new / l_new
            else:
                m_scr[rows(start_q, blk), :] = m_new
                l_scr[rows(start_q, blk), :] = l_new
                acc_scr[rows(start_q, blk), :] = acc_new

        def residue(r, carry, dil=dil, n_blk=n_blk, block=block):
            block(r, r, bias_first)

            def step(c, carry2):
                block(r + dil * blk * c, r + dil * blk * (c - 1), bias_rest)
                return carry2

            lax.fori_loop(1, n_blk, step, 0)
            return carry

        lax.fori_loop(0, dil, residue, 0)


def _dilated_attention(qa, ka, va):
    b, s, d = qa.shape
    spec = pl.BlockSpec((None, s, LANES), lambda i, j: (i, 0, j))
    return pl.pallas_call(
        functools.partial(_dilated_kernel, seq=s),
        grid=(b, d // LANES),
        in_specs=[spec, spec, spec],
        out_specs=spec,
        out_shape=jax.ShapeDtypeStruct((b, s, d), F32),
        scratch_shapes=[pltpu.VMEM((s, LANES), F32)] * 3,
        compiler_params=_params("parallel", "parallel"),
        name="dilated_attention",
    )(qa, ka, va)


def _stick_breaking_kernel(q_ref, k_ref, v_ref, o_ref):
    blk = ATT_BLOCK
    qi = pl.program_id(2)
    q = q_ref[...]
    lane = lax.broadcasted_iota(jnp.int32, (1, LANES), 1)
    row = lax.broadcasted_iota(jnp.int32, (blk, blk), 0)
    col = lax.broadcasted_iota(jnp.int32, (blk, blk), 1)
    suffix_and_total = jnp.concatenate(
        [jnp.where(row > col, 1.0, 0.0), jnp.ones((blk, blk), F32)], axis=1).astype(BF16)
    causal = col < row

    def tile(qh, kb, vb, carry, acc, diagonal):
        z = _dot_nt(qh, kb)
        softplus = jnp.maximum(z, 0.0) + jnp.log1p(jnp.exp(-jnp.abs(z)))
        log_beta = z - softplus
        log_1m = -softplus
        if diagonal:
            log_1m = jnp.where(causal, log_1m, 0.0)
        hi, lo = _split_bf16(log_1m)
        sums = _dot(hi, suffix_and_total) + _dot(lo, suffix_and_total)
        a = jnp.exp(log_beta + carry + sums[:, :blk])
        if diagonal:
            a = jnp.where(causal, a, 0.0)
        return carry + sums[:, blk:], acc + _dot(a.astype(BF16), vb)

    out = None
    for h in range(HEADS_PER_BLOCK):
        hm = (lane // HEAD_DIM) == h
        qh = jnp.where(hm, q, jnp.zeros_like(q))
        zeros = jnp.zeros((blk, blk), F32)
        start = pl.multiple_of(qi * blk, blk)
        carry, acc = tile(qh, k_ref[pl.ds(start, blk), :], v_ref[pl.ds(start, blk), :],
                          zeros, zeros, True)

        def step(t, state, qh=qh):
            start = pl.multiple_of((qi - 1 - t) * blk, blk)
            return tile(qh, k_ref[pl.ds(start, blk), :], v_ref[pl.ds(start, blk), :],
                        state[0], state[1], False)

        carry, acc = lax.fori_loop(0, qi, step, (carry, acc))
        out = acc if out is None else jnp.where(hm, acc, out)
    o_ref[...] = out


def _stick_breaking_attention(qb, kb, vb):
    b, s, d = qb.shape
    blk = ATT_BLOCK
    kv_spec = pl.BlockSpec((None, s, LANES), lambda i, j, t: (i, 0, j))
    q_spec = pl.BlockSpec((None, blk, LANES), lambda i, j, t: (i, t, j))
    return pl.pallas_call(
        _stick_breaking_kernel,
        grid=(b, d // LANES, s // blk),
        in_specs=[q_spec, kv_spec, kv_spec],
        out_specs=q_spec,
        out_shape=jax.ShapeDtypeStruct((b, s, d), F32),
        compiler_params=_params("parallel", "parallel", "arbitrary"),
        name="stick_breaking_attention",
    )(qb, kb, vb)


def _mem_kv_kernel(mem_ref, g_ref, w_ref, kg_ref, k_ref, v_ref):
    mn = _rms(mem_ref[...], g_ref[...]).astype(BF16)
    kv = _dot(mn, w_ref[...])
    d_att = MEM_HEADS * MEM_HEAD_DIM
    ks = []
    for h in range(MEM_HEADS):
        ks.append(_rms(kv[:, h * MEM_HEAD_DIM:(h + 1) * MEM_HEAD_DIM], kg_ref[...]))
    k_ref[...] = jnp.concatenate(ks, axis=1).astype(BF16)
    v_ref[...] = kv[:, d_att:].astype(BF16)


def _mem_kv(mem, g, w_kv, kg):
    b, m, d = mem.shape
    d_att = MEM_HEADS * MEM_HEAD_DIM
    fixed = lambda i: (0, 0)
    out = jax.ShapeDtypeStruct((b, m, d_att), BF16)
    return pl.pallas_call(
        _mem_kv_kernel,
        grid=(b,),
        in_specs=[
            pl.BlockSpec((None, m, d), lambda i: (i, 0, 0)),
            pl.BlockSpec((1, d), fixed),
            pl.BlockSpec(w_kv.shape, fixed),
            pl.BlockSpec((1, MEM_HEAD_DIM), fixed),
        ],
        out_specs=[pl.BlockSpec((None, m, d_att), lambda i: (i, 0, 0))] * 2,
        out_shape=[out, out],
        compiler_params=_params("parallel"),
        name="mem_kv",
    )(mem, g, w_kv, kg)


def _post_kernel(x_ref, oa_ref, ob_ref, ga_ref, gb_ref, wout_ref, gmem_ref, wq_ref,
                 qg_ref, km_ref, vm_ref, wo_ref, gffn_ref, h_ref, hn_ref):
    mixed = jnp.concatenate(
        [_rms(oa_ref[...], ga_ref[...]), _rms(ob_ref[...], gb_ref[...])], axis=1).astype(BF16)
    h1 = x_ref[...] + _dot(mixed, wout_ref[...])

    qm = _dot(_rms(h1, gmem_ref[...]).astype(BF16), wq_ref[...])
    scale = MEM_HEAD_DIM ** -0.5
    heads = []
    for h in range(MEM_HEADS):
        sl = slice(h * MEM_HEAD_DIM, (h + 1) * MEM_HEAD_DIM)
        qh = (_rms(qm[:, sl], qg_ref[...]) * scale).astype(BF16)
        s = _dot_nt(qh, km_ref[:, sl])
        p = jnp.exp(s - jnp.max(s, axis=1, keepdims=True))
        o = _dot(p.astype(BF16), vm_ref[:, sl]) / jnp.sum(p, axis=1, keepdims=True)
        heads.append(o)
    h2 = h1 + _dot(jnp.concatenate(heads, axis=1).astype(BF16), wo_ref[...])
    h_ref[...] = h2
    hn_ref[...] = _rms(h2, gffn_ref[...]).astype(BF16)


def _post(x, oa, ob, ga, gb, w_out, gmem, wq, qg, km, vm, wo, gffn, *, block_rows):
    b, s, d = x.shape
    d_a, d_b = oa.shape[2], ob.shape[2]
    m, d_att = km.shape[1], km.shape[2]
    tok = lambda i, j: (i, j, 0)
    fixed = lambda i, j: (0, 0)
    per_batch = lambda i, j: (i, 0, 0)
    return pl.pallas_call(
        _post_kernel,
        grid=(b, s // block_rows),
        in_specs=[
            pl.BlockSpec((None, block_rows, d), tok),
            pl.BlockSpec((None, block_rows, d_a), tok),
            pl.BlockSpec((None, block_rows, d_b), tok),
            pl.BlockSpec((1, d_a), fixed),
            pl.BlockSpec((1, d_b), fixed),
            pl.BlockSpec(w_out.shape, fixed),
            pl.BlockSpec((1, d), fixed),
            pl.BlockSpec(wq.shape, fixed),
            pl.BlockSpec((1, MEM_HEAD_DIM), fixed),
            pl.BlockSpec((None, m, d_att), per_batch),
            pl.BlockSpec((None, m, d_att), per_batch),
            pl.BlockSpec(wo.shape, fixed),
            pl.BlockSpec((1, d), fixed),
        ],
        out_specs=[pl.BlockSpec((None, block_rows, d), tok)] * 2,
        out_shape=[jax.ShapeDtypeStruct((b, s, d), F32), jax.ShapeDtypeStruct((b, s, d), BF16)],
        compiler_params=_params("parallel", "parallel"),
        name="post_mixer",
    )(x, oa, ob, ga, gb, w_out, gmem, wq, qg, km, vm, wo, gffn)


def _bitonic_merge_desc(vals):
    vals = list(vals)
    n = len(vals)
    j = n // 2
    while j >= 1:
        for i in range(n):
            partner = i ^ j
            if partner > i:
                hi = jnp.maximum(vals[i], vals[partner])
                lo = jnp.minimum(vals[i], vals[partner])
                vals[i], vals[partner] = hi, lo
        j //= 2
    return vals


def _sort_desc(vals):
    vals = list(vals)
    n = len(vals)
    k = 2
    while k <= n:
        j = k // 2
        while j >= 1:
            for i in range(n):
                partner = i ^ j
                if partner > i:
                    hi = jnp.maximum(vals[i], vals[partner])
                    lo = jnp.minimum(vals[i], vals[partner])
                    if (i & k) == 0:
                        vals[i], vals[partner] = hi, lo
                    else:
                        vals[i], vals[partner] = lo, hi
            j //= 2
        k *= 2
    return vals


def _top16_over_rows(x):
    n_vreg_rows = x.shape[0] // SUBLANES
    assert n_vreg_rows == PEER_TOPK
    vals = _sort_desc([x[SUBLANES * v:SUBLANES * (v + 1), :] for v in range(n_vreg_rows)])
    shift = SUBLANES // 2
    while shift >= 1:
        other = [pltpu.roll(vals[PEER_TOPK - 1 - k], shift, 0) for k in range(PEER_TOPK)]
        vals = _bitonic_merge_desc([jnp.maximum(vals[k], other[k]) for k in range(PEER_TOPK)])
        shift //= 2
    return [v[0:1, :] for v in vals]


def _peer_select_kernel(hn_ref, wq_ref, keys_ref, e1_ref, e2_ref, tau_ref):
    hn = hn_ref[...]
    dk = PEER_NKEYS
    top1, top2 = [], []
    for h in range(PEER_HEADS):
        qt = _dot_nt(wq_ref[2 * dk * h:2 * dk * (h + 1), :], hn).astype(BF16)
        s1 = _dot(keys_ref[2 * h], qt[:dk])
        s2 = _dot(keys_ref[2 * h + 1], qt[dk:])
        e1 = jnp.exp(s1 - jnp.max(s1, axis=0, keepdims=True))
        e2 = jnp.exp(s2 - jnp.max(s2, axis=0, keepdims=True))
        e1_ref[h] = e1
        e2_ref[h] = e2
        top1.append(_top16_over_rows(e1))
        top2.append(_top16_over_rows(e2))
    ea = [jnp.concatenate([top1[h][k] for h in range(PEER_HEADS)], axis=0) for k in range(PEER_TOPK)]
    eb = [jnp.concatenate([top2[h][k] for h in range(PEER_HEADS)], axis=0) for k in range(PEER_TOPK)]
    pairs = [(k, l) for k in range(PEER_TOPK) for l in range(PEER_TOPK) if (k + 1) * (l + 1) <= PEER_TOPK]
    cand = [ea[k] * eb[l] for k, l in pairs]
    work = list(cand)
    z = jnp.zeros_like(cand[0])
    kth = None
    for _ in range(PEER_TOPK):
        kth = functools.reduce(jnp.maximum, work)
        z = z + kth
        taken = jnp.zeros(kth.shape, jnp.bool_)
        for n in range(len(work)):
            hit = jnp.logical_and(work[n] == kth, jnp.logical_not(taken))
            work[n] = jnp.where(hit, -1.0, work[n])
            taken = jnp.logical_or(taken, hit)
    rz = 1.0 / z
    tau = None
    for (k, l), c in zip(pairs, cand):
        scaled = jnp.where(c >= kth, (ea[k] * rz) * eb[l], jnp.inf)
        tau = scaled if tau is None else jnp.minimum(tau, scaled)
    tau_ref[...] = tau
    for h in range(PEER_HEADS):
        e1_ref[h] = e1_ref[h] * rz[h:h + 1, :]


def _peer_select(hn2, wq_t, keys, *, block_tokens):
    n, d = hn2.shape
    fixed2 = lambda i: (0, 0)
    e_shape = jax.ShapeDtypeStruct((PEER_HEADS, PEER_NKEYS, n), F32)
    e_spec = pl.BlockSpec((PEER_HEADS, PEER_NKEYS, block_tokens), lambda i: (0, 0, i))
    return pl.pallas_call(
        _peer_select_kernel,
        grid=(n // block_tokens,),
        in_specs=[
            pl.BlockSpec((block_tokens, d), lambda i: (i, 0)),
            pl.BlockSpec(wq_t.shape, fixed2),
            pl.BlockSpec(keys.shape, lambda i: (0, 0, 0)),
        ],
        out_specs=[e_spec, e_spec, pl.BlockSpec((PEER_HEADS, block_tokens), lambda i: (0, i))],
        out_shape=[e_shape, e_shape, jax.ShapeDtypeStruct((PEER_HEADS, n), F32)],
        compiler_params=_params("parallel"),
        name="peer_select",
    )(hn2, wq_t, keys)


def _peer_dense_kernel(hn_ref, h_ref, e1_ref, e2_ref, tau_ref, u_ref, vt_ref, o_ref,
                       a_scr, z_scr, y_scr, *, rows_per_step):
    eb = pl.program_id(1)
    n_tok = hn_ref.shape[0]

    @pl.when(eb == 0)
    def _():
        y_scr[...] = jnp.zeros(y_scr.shape, F32)

    a_scr[...] = _dot_nt(u_ref[...], hn_ref[...])
    row0 = pl.multiple_of(eb * rows_per_step, rows_per_step)

    for tc in range(n_tok // LANES):
        cols = slice(tc * LANES, (tc + 1) * LANES)
        e1_rows = [e1_ref[h, pl.ds(row0, rows_per_step), cols] for h in range(PEER_HEADS)]
        taus = [tau_ref[h:h + 1, cols] for h in range(PEER_HEADS)]
        for ii in range(rows_per_step):
            rows = slice(ii * PEER_NKEYS, (ii + 1) * PEER_NKEYS)
            w = jnp.zeros((PEER_NKEYS, LANES), F32)
            for h in range(PEER_HEADS):
                p = e1_rows[h][ii:ii + 1, :] * e2_ref[h, :, cols]
                w = w + jnp.where(p >= taus[h], p, 0.0)
            a = a_scr[rows, cols]
            gelu = 0.5 * a * (1.0 + lax.erf(a * INV_SQRT2))
            z_scr[rows, cols] = (w * gelu).astype(BF16)
    y_scr[...] += _dot(vt_ref[...], z_scr[...])

    @pl.when(eb == pl.num_programs(1) - 1)
    def _():
        o_ref[...] = h_ref[...] + y_scr[...].T


def _peer_dense(hn2, h2, e1, e2, tau, u, v_t, *, block_tokens, rows_per_step):
    n, d = hn2.shape
    n_exp = u.shape[0]
    block_exp = rows_per_step * PEER_NKEYS
    tok = lambda i, j: (i, 0)
    e_spec = pl.BlockSpec((PEER_HEADS, PEER_NKEYS, block_tokens), lambda i, j: (0, 0, i))
    return pl.pallas_call(
        functools.partial(_peer_dense_kernel, rows_per_step=rows_per_step),
        grid=(n // block_tokens, n_exp // block_exp),
        in_specs=[
            pl.BlockSpec((block_tokens, d), tok),
            pl.BlockSpec((block_tokens, d), tok),
            e_spec,
            e_spec,
            pl.BlockSpec((PEER_HEADS, block_tokens), lambda i, j: (0, i)),
            pl.BlockSpec((block_exp, d), lambda i, j: (j, 0)),
            pl.BlockSpec((d, block_exp), lambda i, j: (0, j)),
        ],
        out_specs=pl.BlockSpec((block_tokens, d), tok),
        out_shape=jax.ShapeDtypeStruct((n, d), F32),
        scratch_shapes=[
            pltpu.VMEM((block_exp, block_tokens), F32),
            pltpu.VMEM((block_exp, block_tokens), BF16),
            pltpu.VMEM((d, block_tokens), F32),
        ],
        compiler_params=_params("parallel", "arbitrary"),
        name="peer_dense",
    )(hn2, h2, e1, e2, tau, u, v_t)


def _rope_lane_frequencies():
    half = ROPE_DIM // 2
    inv_freq = ROPE_THETA ** (-jnp.arange(0, ROPE_DIM, 2, dtype=F32) / ROPE_DIM)
    per_head = jnp.concatenate([inv_freq, inv_freq, jnp.zeros((HEAD_DIM - 2 * half,), F32)])
    return jnp.tile(per_head, HEADS_PER_BLOCK)[None, :]


def _layer(h, mem, positions, norm_mix_g, w_in, qnorm_g_dil, knorm_g_dil, out_norm_g_dil,
           out_norm_g_sb, w_out, norm_mem_g, norm_memtok_g, w_q_mem, w_kv_mem, qnorm_g_mem,
           knorm_g_mem, w_o_mem, norm_ffn_g, w_q_peer, peer_sub_keys, peer_u, peer_v):
    b, s, d = h.shape
    n = b * s
    d_grp = w_in.shape[1] // 6
    row = lambda g: g[None, :]
    pair = lambda g: jnp.tile(g, HEADS_PER_BLOCK)[None, :]

    qa, ka, va, qb, kb, vb = _in_proj(
        h.reshape(n, d), positions.reshape(n, 1), row(norm_mix_g), w_in.astype(BF16),
        pair(qnorm_g_dil), pair(knorm_g_dil), _rope_lane_frequencies(), block_rows=512)
    shape3 = (b, s, d_grp)
    o_a = _dilated_attention(qa.reshape(shape3), ka.reshape(shape3), va.reshape(shape3))
    o_b = _stick_breaking_attention(qb.reshape(shape3), kb.reshape(shape3), vb.reshape(shape3))

    km, vm = _mem_kv(mem, row(norm_memtok_g), w_kv_mem.astype(BF16), row(knorm_g_mem))
    h2, hn = _post(h, o_a, o_b, row(out_norm_g_dil), row(out_norm_g_sb), w_out.astype(BF16),
                   row(norm_mem_g), w_q_mem.astype(BF16), row(qnorm_g_mem), km, vm,
                   w_o_mem.astype(BF16), row(norm_ffn_g), block_rows=256)

    hn2 = hn.reshape(n, d)
    keys = peer_sub_keys.reshape(PEER_HEADS * 2, PEER_NKEYS, -1).astype(BF16)
    e1, e2, tau = _peer_select(hn2, w_q_peer.T.astype(BF16), keys, block_tokens=256)
    out = _peer_dense(hn2, h2.reshape(n, d), e1, e2, tau, peer_u.astype(BF16),
                      peer_v.T.astype(BF16), block_tokens=512, rows_per_step=8)
    return out.reshape(b, s, d)


def kernel(x, mem, positions, norm_mix_g, w_in, qnorm_g_dil, knorm_g_dil, out_norm_g_dil, out_norm_g_sb, w_out, norm_mem_g, norm_memtok_g, w_q_mem, w_kv_mem, qnorm_g_mem, knorm_g_mem, w_o_mem, norm_ffn_g, w_q_peer, peer_sub_keys, peer_u, peer_v):
    h = x
    for l in range(norm_mix_g.shape[0]):
        h = _layer(h, mem, positions, norm_mix_g[l], w_in[l], qnorm_g_dil[l], knorm_g_dil[l],
                   out_norm_g_dil[l], out_norm_g_sb[l], w_out[l], norm_mem_g[l], norm_memtok_g[l],
                   w_q_mem[l], w_kv_mem[l], qnorm_g_mem[l], knorm_g_mem[l], w_o_mem[l],
                   norm_ffn_g[l], w_q_peer[l], peer_sub_keys[l], peer_u[l], peer_v[l])
    return h
```

```python
import functools
import math

import jax
import jax.numpy as jnp
from jax import lax
from jax.experimental import pallas as pl
from jax.experimental.pallas import tpu as pltpu

F32 = jnp.float32
BF16 = jnp.bfloat16

LANES = 128
SUBLANES = 8
VMEM_LIMIT_BYTES = 56 * 1024 * 1024

EPS = 1e-6
HEAD_DIM = 64
HEADS_PER_BLOCK = LANES // HEAD_DIM
DIL_PATTERNS = ((128, 1), (512, 4), (2048, 16))
ATT_BLOCK = 128
DIL_BLOCKS_PER_STEP = 4
ROPE_THETA = 500000.0
ROPE_DIM = HEAD_DIM // 4
MEM_HEADS = 4
MEM_HEAD_DIM = 128
PEER_HEADS = 8
PEER_NKEYS = 128
PEER_TOPK = 16
INV_SQRT2 = 1.0 / math.sqrt(2.0)

NT_DIMS = (((1,), (1,)), ((), ()))


def _rms(x, gain):
    return x * lax.rsqrt(jnp.mean(x * x, axis=-1, keepdims=True) + EPS) * gain


def _dot(a, b):
    return jnp.dot(a, b, preferred_element_type=F32)


def _dot_nt(a, b):
    return lax.dot_general(a, b, NT_DIMS, preferred_element_type=F32)


def _split_bf16(x):
    hi = x.astype(BF16)
    lo = (x - hi.astype(F32)).astype(BF16)
    return hi, lo


def _params(*semantics):
    return pltpu.CompilerParams(dimension_semantics=semantics,
                                vmem_limit_bytes=VMEM_LIMIT_BYTES)


def _in_proj_kernel(x_ref, pos_ref, g_ref, w_ref, qg_ref, kg_ref, invf_ref,
                    qa_ref, ka_ref, va_ref, qb_ref, kb_ref, vb_ref, *, d_grp):
    hb = _rms(x_ref[...], g_ref[...]).astype(BF16)

    def proj(c):
        return _dot(hb, w_ref[:, c * d_grp:(c + 1) * d_grp])

    ang = pos_ref[...].astype(F32) * invf_ref[...]
    cos = jnp.cos(ang)
    sin = jnp.sin(ang)
    lane = lax.broadcasted_iota(jnp.int32, (1, LANES), 1) % HEAD_DIM
    half = ROPE_DIM // 2
    sin_lo = jnp.where(lane < half, -sin, 0.0)
    sin_hi = jnp.where(lane < half, 0.0, sin)
    r = lax.broadcasted_iota(jnp.int32, (LANES, LANES), 0) // HEAD_DIM
    c = lax.broadcasted_iota(jnp.int32, (LANES, LANES), 1) // HEAD_DIM
    head_mean = jnp.where(r == c, 1.0 / HEAD_DIM, 0.0).astype(BF16)

    def qk_norm_rope(t, gain, scale):
        outs = []
        for j in range(d_grp // LANES):
            tc = t[:, j * LANES:(j + 1) * LANES]
            hi, lo = _split_bf16(tc * tc)
            ms = _dot(hi, head_mean) + _dot(lo, head_mean)
            y = tc * lax.rsqrt(ms + EPS) * gain
            y = (y * cos + pltpu.roll(y, LANES - half, 1) * sin_lo
                 + pltpu.roll(y, half, 1) * sin_hi)
            outs.append(y * scale)
        return jnp.concatenate(outs, axis=1)

    scale = HEAD_DIM ** -0.5
    qa_ref[...] = qk_norm_rope(proj(0), qg_ref[...], scale)
    ka_ref[...] = qk_norm_rope(proj(1), kg_ref[...], 1.0)
    va_ref[...] = proj(2)
    qb_ref[...] = (proj(3) * scale).astype(BF16)
    kb_ref[...] = proj(4).astype(BF16)
    vb_ref[...] = proj(5).astype(BF16)


def _in_proj(x2, pos2, g, w_in, qg, kg, invf, *, block_rows):
    n, d = x2.shape
    d_grp = w_in.shape[1] // 6
    row = lambda i: (i, 0)
    fixed = lambda i: (0, 0)
    out_f32 = jax.ShapeDtypeStruct((n, d_grp), F32)
    out_bf16 = jax.ShapeDtypeStruct((n, d_grp), BF16)
    return pl.pallas_call(
        functools.partial(_in_proj_kernel, d_grp=d_grp),
        grid=(n // block_rows,),
        in_specs=[
            pl.BlockSpec((block_rows, d), row),
            pl.BlockSpec((block_rows, 1), row),
            pl.BlockSpec((1, d), fixed),
            pl.BlockSpec(w_in.shape, fixed),
            pl.BlockSpec((1, LANES), fixed),
            pl.BlockSpec((1, LANES), fixed),
            pl.BlockSpec((1, LANES), fixed),
        ],
        out_specs=[pl.BlockSpec((block_rows, d_grp), row)] * 6,
        out_shape=[out_f32, out_f32, out_f32, out_bf16, out_bf16, out_bf16],
        compiler_params=_params("parallel"),
        name="in_proj",
    )(x2, pos2, g, w_in, qg, kg, invf)


def _dilated_kernel(q_ref, k_ref, v_ref, o_ref, m_scr, l_scr, acc_scr, *, seq):
    blk = ATT_BLOCK
    m_scr[...] = jnp.full(m_scr.shape, -jnp.inf, F32)
    l_scr[...] = jnp.zeros(l_scr.shape, F32)
    acc_scr[...] = jnp.zeros(acc_scr.shape, F32)

    lane = lax.broadcasted_iota(jnp.int32, (1, LANES), 1)
    head_masks = [(lane // HEAD_DIM) == h for h in range(HEADS_PER_BLOCK)]
    rel = (lax.broadcasted_iota(jnp.int32, (blk, 2 * blk), 0)
           - lax.broadcasted_iota(jnp.int32, (blk, 2 * blk), 1))
    bias_first = jnp.where(rel >= 0, 0.0, -jnp.inf).astype(F32)
    bias_rest = jnp.where((rel + blk >= 0) & (rel <= 0), 0.0, -jnp.inf).astype(F32)

    for p_idx, (window, dil) in enumerate(DIL_PATTERNS):
        assert window // dil == blk
        n_blk = seq // (dil * blk)
        assert n_blk >= 2
        last_pattern = p_idx == len(DIL_PATTERNS) - 1

        def rows(start, size, dil=dil):
            return pl.ds(start, size, stride=dil) if dil > 1 else pl.ds(start, size)

        def block(start_q, start_k, bias, last_pattern=last_pattern, rows=rows):
            q = q_ref[rows(start_q, blk), :]
            k = k_ref[rows(start_k, 2 * blk), :].astype(BF16)
            v = v_ref[rows(start_k, 2 * blk), :].astype(BF16)
            m_b = l_b = pv_b = None
            for hm in head_masks:
                qh = jnp.where(hm, q, 0.0).astype(BF16)
                s = _dot_nt(qh, k) + bias
                m_h = jnp.max(s, axis=1, keepdims=True)
                p = jnp.exp(s - m_h)
                l_h = jnp.sum(p, axis=1, keepdims=True)
                pv_h = _dot(p.astype(BF16), v)
                if m_b is None:
                    m_b, l_b, pv_b = m_h, l_h, pv_h
                else:
                    m_b = jnp.where(hm, m_h, m_b)
                    l_b = jnp.where(hm, l_h, l_b)
                    pv_b = jnp.where(hm, pv_h, pv_b)
            m_old = m_scr[rows(start_q, blk), :]
            l_old = l_scr[rows(start_q, blk), :]
            acc_old = acc_scr[rows(start_q, blk), :]
            m_new = jnp.maximum(m_old, m_b)
            a_old = jnp.exp(m_old - m_new)
            a_blk = jnp.exp(m_b - m_new)
            l_new = a_old * l_old + a_blk * l_b
            acc_new = a_old * acc_old + a_blk * pv_b
            if last_pattern:
                o_ref[rows(start_q, blk), :] = acc_new / l_new
            else:
                m_scr[rows(start_q, blk), :] = m_new
                l_scr[rows(start_q, blk), :] = l_new
                acc_scr[rows(start_q, blk), :] = acc_new

        n_total = dil * n_blk
        assert n_total % DIL_BLOCKS_PER_STEP == 0

        def group(g, carry, dil=dil, n_blk=n_blk, block=block):
            for u in range(DIL_BLOCKS_PER_STEP):
                idx = g * DIL_BLOCKS_PER_STEP + u
                r = idx // n_blk
                c = idx % n_blk
                first = c == 0
                start_k = r + dil * blk * jnp.maximum(c - 1, 0)
                block(r + dil * blk * c, start_k, jnp.where(first, bias_first, bias_rest))
            return carry

        lax.fori_loop(0, n_total // DIL_BLOCKS_PER_STEP, group, 0)


def _dilated_attention(qa, ka, va):
    b, s, d = qa.shape
    spec = pl.BlockSpec((None, s, LANES), lambda i, j: (i, 0, j))
    return pl.pallas_call(
        functools.partial(_dilated_kernel, seq=s),
        grid=(b, d // LANES),
        in_specs=[spec, spec, spec],
        out_specs=spec,
        out_shape=jax.ShapeDtypeStruct((b, s, d), F32),
        scratch_shapes=[pltpu.VMEM((s, LANES), F32)] * 3,
        compiler_params=_params("parallel", "parallel"),
        name="dilated_attention",
    )(qa, ka, va)


def _stick_breaking_kernel(q_ref, k_ref, v_ref, o_ref, carry_scr, acc_scr, *, q_rows):
    blk = ATT_BLOCK
    n_sub = q_rows // blk
    n_heads = HEADS_PER_BLOCK
    qi = pl.program_id(2)
    lane = lax.broadcasted_iota(jnp.int32, (1, LANES), 1)
    head_masks = [(lane // HEAD_DIM) == h for h in range(n_heads)]
    row = lax.broadcasted_iota(jnp.int32, (blk, blk), 0)
    col = lax.broadcasted_iota(jnp.int32, (blk, blk), 1)
    suffix_and_total = jnp.concatenate(
        [jnp.where(row > col, 1.0, 0.0), jnp.ones((blk, blk), F32)], axis=1).astype(BF16)

    carry_scr[...] = jnp.zeros(carry_scr.shape, F32)
    acc_scr[...] = jnp.zeros(acc_scr.shape, F32)

    def per_head_rows(x):
        return jnp.concatenate([jnp.where(hm, x, jnp.zeros_like(x)) for hm in head_masks], axis=0)

    def sweep(key_start, first_row, masked):
        n_rows = q_rows - first_row
        k_heads = per_head_rows(k_ref[pl.ds(key_start, blk), :])
        v_heads = per_head_rows(v_ref[pl.ds(key_start, blk), :])
        z_all = _dot_nt(q_ref[first_row:, :], k_heads)
        if masked:
            causal = (lax.broadcasted_iota(jnp.int32, (n_rows, blk), 1)
                      < lax.broadcasted_iota(jnp.int32, (n_rows, blk), 0))
        weights = []
        for h in range(n_heads):
            z = z_all[:, h * blk:(h + 1) * blk]
            softplus = jnp.maximum(z, 0.0) + jnp.log(1.0 + jnp.exp(-jnp.abs(z)))
            log_1m = -softplus
            if masked:
                log_1m = jnp.where(causal, log_1m, 0.0)
            sums = _dot(log_1m.astype(BF16), suffix_and_total)
            a = jnp.exp((z - softplus) + carry_scr[h, first_row:, :] + sums[:, :blk])
            if masked:
                a = jnp.where(causal, a, 0.0)
            carry_scr[h, first_row:, :] += sums[:, blk:]
            weights.append(a.astype(BF16))
        acc_scr[first_row:, :] += _dot(jnp.concatenate(weights, axis=1), v_heads)

    for jb in reversed(range(n_sub)):
        sweep(pl.multiple_of((qi * n_sub + jb) * blk, blk), jb * blk, True)

    assert n_sub % 2 == 0

    def step(t, carry):
        for u in range(2):
            sweep(pl.multiple_of((qi * n_sub - 1 - 2 * t - u) * blk, blk), 0, False)
        return carry

    lax.fori_loop(0, qi * (n_sub // 2), step, 0)
    o_ref[...] = acc_scr[...]


def _stick_breaking_attention(qb, kb, vb, *, q_rows):
    b, s, d = qb.shape
    kv_spec = pl.BlockSpec((None, s, LANES), lambda i, j, t: (i, 0, j))
    q_spec = pl.BlockSpec((None, q_rows, LANES), lambda i, j, t: (i, t, j))
    return pl.pallas_call(
        functools.partial(_stick_breaking_kernel, q_rows=q_rows),
        grid=(b, d // LANES, s // q_rows),
        in_specs=[q_spec, kv_spec, kv_spec],
        out_specs=q_spec,
        out_shape=jax.ShapeDtypeStruct((b, s, d), F32),
        scratch_shapes=[pltpu.VMEM((HEADS_PER_BLOCK, q_rows, LANES), F32),
                        pltpu.VMEM((q_rows, LANES), F32)],
        compiler_params=_params("parallel", "parallel", "arbitrary"),
        name="stick_breaking_attention",
    )(qb, kb, vb)


def _mem_kv_kernel(mem_ref, g_ref, w_ref, kg_ref, k_ref, v_ref):
    mn = _rms(mem_ref[...], g_ref[...]).astype(BF16)
    kv = _dot(mn, w_ref[...])
    d_att = MEM_HEADS * MEM_HEAD_DIM
    ks = []
    for h in range(MEM_HEADS):
        ks.append(_rms(kv[:, h * MEM_HEAD_DIM:(h + 1) * MEM_HEAD_DIM], kg_ref[...]))
    k_ref[...] = jnp.concatenate(ks, axis=1).astype(BF16)
    v_ref[...] = kv[:, d_att:].astype(BF16)


def _mem_kv(mem, g, w_kv, kg):
    b, m, d = mem.shape
    d_att = MEM_HEADS * MEM_HEAD_DIM
    fixed = lambda i: (0, 0)
    out = jax.ShapeDtypeStruct((b, m, d_att), BF16)
    return pl.pallas_call(
        _mem_kv_kernel,
        grid=(b,),
        in_specs=[
            pl.BlockSpec((None, m, d), lambda i: (i, 0, 0)),
            pl.BlockSpec((1, d), fixed),
            pl.BlockSpec(w_kv.shape, fixed),
            pl.BlockSpec((1, MEM_HEAD_DIM), fixed),
        ],
        out_specs=[pl.BlockSpec((None, m, d_att), lambda i: (i, 0, 0))] * 2,
        out_shape=[out, out],
        compiler_params=_params("parallel"),
        name="mem_kv",
    )(mem, g, w_kv, kg)


def _post_kernel(x_ref, oa_ref, ob_ref, ga_ref, gb_ref, wout_ref, gmem_ref, wq_ref,
                 qg_ref, km_ref, vm_ref, wo_ref, gffn_ref, h_ref, hn_ref):
    mixed = jnp.concatenate(
        [_rms(oa_ref[...], ga_ref[...]), _rms(ob_ref[...], gb_ref[...])], axis=1).astype(BF16)
    h1 = x_ref[...] + _dot(mixed, wout_ref[...])

    qm = _dot(_rms(h1, gmem_ref[...]).astype(BF16), wq_ref[...])
    scale = MEM_HEAD_DIM ** -0.5
    heads = []
    for h in range(MEM_HEADS):
        sl = slice(h * MEM_HEAD_DIM, (h + 1) * MEM_HEAD_DIM)
        qh = (_rms(qm[:, sl], qg_ref[...]) * scale).astype(BF16)
        s = _dot_nt(qh, km_ref[:, sl])
        p = jnp.exp(s - jnp.max(s, axis=1, keepdims=True))
        o = _dot(p.astype(BF16), vm_ref[:, sl]) / jnp.sum(p, axis=1, keepdims=True)
        heads.append(o)
    h2 = h1 + _dot(jnp.concatenate(heads, axis=1).astype(BF16), wo_ref[...])
    h_ref[...] = h2
    hn_ref[...] = _rms(h2, gffn_ref[...]).astype(BF16)


def _post(x, oa, ob, ga, gb, w_out, gmem, wq, qg, km, vm, wo, gffn, *, block_rows):
    b, s, d = x.shape
    d_a, d_b = oa.shape[2], ob.shape[2]
    m, d_att = km.shape[1], km.shape[2]
    tok = lambda i, j: (i, j, 0)
    fixed = lambda i, j: (0, 0)
    per_batch = lambda i, j: (i, 0, 0)
    return pl.pallas_call(
        _post_kernel,
        grid=(b, s // block_rows),
        in_specs=[
            pl.BlockSpec((None, block_rows, d), tok),
            pl.BlockSpec((None, block_rows, d_a), tok),
            pl.BlockSpec((None, block_rows, d_b), tok),
            pl.BlockSpec((1, d_a), fixed),
            pl.BlockSpec((1, d_b), fixed),
            pl.BlockSpec(w_out.shape, fixed),
            pl.BlockSpec((1, d), fixed),
            pl.BlockSpec(wq.shape, fixed),
            pl.BlockSpec((1, MEM_HEAD_DIM), fixed),
            pl.BlockSpec((None, m, d_att), per_batch),
            pl.BlockSpec((None, m, d_att), per_batch),
            pl.BlockSpec(wo.shape, fixed),
            pl.BlockSpec((1, d), fixed),
        ],
        out_specs=[pl.BlockSpec((None, block_rows, d), tok)] * 2,
        out_shape=[jax.ShapeDtypeStruct((b, s, d), F32), jax.ShapeDtypeStruct((b, s, d), BF16)],
        compiler_params=_params("parallel", "parallel"),
        name="post_mixer",
    )(x, oa, ob, ga, gb, w_out, gmem, wq, qg, km, vm, wo, gffn)


def _bitonic_merge_desc(vals):
    vals = list(vals)
    n = len(vals)
    j = n // 2
    while j >= 1:
        for i in range(n):
            partner = i ^ j
            if partner > i:
                hi = jnp.maximum(vals[i], vals[partner])
                lo = jnp.minimum(vals[i], vals[partner])
                vals[i], vals[partner] = hi, lo
        j //= 2
    return vals


def _sort_desc(vals):
    vals = list(vals)
    n = len(vals)
    k = 2
    while k <= n:
        j = k // 2
        while j >= 1:
            for i in range(n):
                partner = i ^ j
                if partner > i:
                    hi = jnp.maximum(vals[i], vals[partner])
                    lo = jnp.minimum(vals[i], vals[partner])
                    if (i & k) == 0:
                        vals[i], vals[partner] = hi, lo
                    else:
                        vals[i], vals[partner] = lo, hi
            j //= 2
        k *= 2
    return vals


def _top16_over_rows(x):
    n_vreg_rows = x.shape[0] // SUBLANES
    assert n_vreg_rows == PEER_TOPK
    vals = _sort_desc([x[SUBLANES * v:SUBLANES * (v + 1), :] for v in range(n_vreg_rows)])
    shift = SUBLANES // 2
    while shift >= 1:
        other = [pltpu.roll(vals[PEER_TOPK - 1 - k], shift, 0) for k in range(PEER_TOPK)]
        vals = _bitonic_merge_desc([jnp.maximum(vals[k], other[k]) for k in range(PEER_TOPK)])
        shift //= 2
    return [v[0:1, :] for v in vals]


def _peer_select_kernel(hn_ref, wq_ref, keys_ref, e1_ref, e2_ref, tau_ref):
    hn = hn_ref[...]
    dk = PEER_NKEYS
    top1, top2 = [], []
    for h in range(PEER_HEADS):
        qt = _dot_nt(wq_ref[2 * dk * h:2 * dk * (h + 1), :], hn).astype(BF16)
        s1 = _dot(keys_ref[2 * h], qt[:dk])
        s2 = _dot(keys_ref[2 * h + 1], qt[dk:])
        e1 = jnp.exp(s1 - jnp.max(s1, axis=0, keepdims=True))
        e2 = jnp.exp(s2 - jnp.max(s2, axis=0, keepdims=True))
        e1_ref[h] = e1
        e2_ref[h] = e2
        top1.append(_top16_over_rows(e1))
        top2.append(_top16_over_rows(e2))
    ea = [jnp.concatenate([top1[h][k] for h in range(PEER_HEADS)], axis=0) for k in range(PEER_TOPK)]
    eb = [jnp.concatenate([top2[h][k] for h in range(PEER_HEADS)], axis=0) for k in range(PEER_TOPK)]
    pairs = [(k, l) for k in range(PEER_TOPK) for l in range(PEER_TOPK) if (k + 1) * (l + 1) <= PEER_TOPK]
    cand = [ea[k] * eb[l] for k, l in pairs]
    work = list(cand)
    z = jnp.zeros_like(cand[0])
    kth = None
    for _ in range(PEER_TOPK):
        kth = functools.reduce(jnp.maximum, work)
        z = z + kth
        taken = jnp.zeros(kth.shape, jnp.bool_)
        for n in range(len(work)):
            hit = jnp.logical_and(work[n] == kth, jnp.logical_not(taken))
            work[n] = jnp.where(hit, -1.0, work[n])
            taken = jnp.logical_or(taken, hit)
    rz = 1.0 / z
    tau = None
    for (k, l), c in zip(pairs, cand):
        scaled = jnp.where(c >= kth, (ea[k] * rz) * eb[l], jnp.inf)
        tau = scaled if tau is None else jnp.minimum(tau, scaled)
    tau_ref[...] = tau
    for h in range(PEER_HEADS):
        e1_ref[h] = e1_ref[h] * rz[h:h + 1, :]


def _peer_select(hn2, wq_t, keys, *, block_tokens):
    n, d = hn2.shape
    fixed2 = lambda i: (0, 0)
    e_shape = jax.ShapeDtypeStruct((PEER_HEADS, PEER_NKEYS, n), F32)
    e_spec = pl.BlockSpec((PEER_HEADS, PEER_NKEYS, block_tokens), lambda i: (0, 0, i))
    return pl.pallas_call(
        _peer_select_kernel,
        grid=(n // block_tokens,),
        in_specs=[
            pl.BlockSpec((block_tokens, d), lambda i: (i, 0)),
            pl.BlockSpec(wq_t.shape, fixed2),
            pl.BlockSpec(keys.shape, lambda i: (0, 0, 0)),
        ],
        out_specs=[e_spec, e_spec, pl.BlockSpec((PEER_HEADS, block_tokens), lambda i: (0, i))],
        out_shape=[e_shape, e_shape, jax.ShapeDtypeStruct((PEER_HEADS, n), F32)],
        compiler_params=_params("parallel"),
        name="peer_select",
    )(hn2, wq_t, keys)


def _peer_dense_kernel(hn_ref, h_ref, e1_ref, e2_ref, tau_ref, u_ref, vt_ref, o_ref,
                       a_scr, z_scr, y_scr, *, rows_per_step):
    eb = pl.program_id(1)
    n_tok = hn_ref.shape[0]

    @pl.when(eb == 0)
    def _():
        y_scr[...] = jnp.zeros(y_scr.shape, F32)

    a_scr[...] = _dot_nt(u_ref[...], hn_ref[...])
    row0 = pl.multiple_of(eb * rows_per_step, rows_per_step)

    for tc in range(n_tok // LANES):
        cols = slice(tc * LANES, (tc + 1) * LANES)
        e1_rows = [e1_ref[h, pl.ds(row0, rows_per_step), cols] for h in range(PEER_HEADS)]
        taus = [tau_ref[h:h + 1, cols] for h in range(PEER_HEADS)]
        for ii in range(rows_per_step):
            rows = slice(ii * PEER_NKEYS, (ii + 1) * PEER_NKEYS)
            w = jnp.zeros((PEER_NKEYS, LANES), F32)
            for h in range(PEER_HEADS):
                p = e1_rows[h][ii:ii + 1, :] * e2_ref[h, :, cols]
                w = w + jnp.where(p >= taus[h], p, 0.0)
            a = a_scr[rows, cols]
            gelu = 0.5 * a * (1.0 + lax.erf(a * INV_SQRT2))
            z_scr[rows, cols] = (w * gelu).astype(BF16)
    y_scr[...] += _dot(vt_ref[...], z_scr[...])

    @pl.when(eb == pl.num_programs(1) - 1)
    def _():
        o_ref[...] = h_ref[...] + y_scr[...].T


def _peer_dense(hn2, h2, e1, e2, tau, u, v_t, *, block_tokens, rows_per_step):
    n, d = hn2.shape
    n_exp = u.shape[0]
    block_exp = rows_per_step * PEER_NKEYS
    tok = lambda i, j: (i, 0)
    e_spec = pl.BlockSpec((PEER_HEADS, PEER_NKEYS, block_tokens), lambda i, j: (0, 0, i))
    return pl.pallas_call(
        functools.partial(_peer_dense_kernel, rows_per_step=rows_per_step),
        grid=(n // block_tokens, n_exp // block_exp),
        in_specs=[
            pl.BlockSpec((block_tokens, d), tok),
            pl.BlockSpec((block_tokens, d), tok),
            e_spec,
            e_spec,
            pl.BlockSpec((PEER_HEADS, block_tokens), lambda i, j: (0, i)),
            pl.BlockSpec((block_exp, d), lambda i, j: (j, 0)),
            pl.BlockSpec((d, block_exp), lambda i, j: (0, j)),
        ],
        out_specs=pl.BlockSpec((block_tokens, d), tok),
        out_shape=jax.ShapeDtypeStruct((n, d), F32),
        scratch_shapes=[
            pltpu.VMEM((block_exp, block_tokens), F32),
            pltpu.VMEM((block_exp, block_tokens), BF16),
            pltpu.VMEM((d, block_tokens), F32),
        ],
        compiler_params=_params("parallel", "arbitrary"),
        name="peer_dense",
    )(hn2, h2, e1, e2, tau, u, v_t)


def _rope_lane_frequencies():
    half = ROPE_DIM // 2
    inv_freq = ROPE_THETA ** (-jnp.arange(0, ROPE_DIM, 2, dtype=F32) / ROPE_DIM)
    per_head = jnp.concatenate([inv_freq, inv_freq, jnp.zeros((HEAD_DIM - 2 * half,), F32)])
    return jnp.tile(per_head, HEADS_PER_BLOCK)[None, :]


def _layer(h, mem, positions, norm_mix_g, w_in, qnorm_g_dil, knorm_g_dil, out_norm_g_dil,
           out_norm_g_sb, w_out, norm_mem_g, norm_memtok_g, w_q_mem, w_kv_mem, qnorm_g_mem,
           knorm_g_mem, w_o_mem, norm_ffn_g, w_q_peer, peer_sub_keys, peer_u, peer_v):
    b, s, d = h.shape
    n = b * s
    d_grp = w_in.shape[1] // 6
    row = lambda g: g[None, :]
    pair = lambda g: jnp.tile(g, HEADS_PER_BLOCK)[None, :]

    qa, ka, va, qb, kb, vb = _in_proj(
        h.reshape(n, d), positions.reshape(n, 1), row(norm_mix_g), w_in.astype(BF16),
        pair(qnorm_g_dil), pair(knorm_g_dil), _rope_lane_frequencies(), block_rows=512)
    shape3 = (b, s, d_grp)
    o_a = _dilated_attention(qa.reshape(shape3), ka.reshape(shape3), va.reshape(shape3))
    o_b = _stick_breaking_attention(qb.reshape(shape3), kb.reshape(shape3), vb.reshape(shape3),
                                    q_rows=512)

    km, vm = _mem_kv(mem, row(norm_memtok_g), w_kv_mem.astype(BF16), row(knorm_g_mem))
    h2, hn = _post(h, o_a, o_b, row(out_norm_g_dil), row(out_norm_g_sb), w_out.astype(BF16),
                   row(norm_mem_g), w_q_mem.astype(BF16), row(qnorm_g_mem), km, vm,
                   w_o_mem.astype(BF16), row(norm_ffn_g), block_rows=256)

    hn2 = hn.reshape(n, d)
    keys = peer_sub_keys.reshape(PEER_HEADS * 2, PEER_NKEYS, -1).astype(BF16)
    e1, e2, tau = _peer_select(hn2, w_q_peer.T.astype(BF16), keys, block_tokens=256)
    out = _peer_dense(hn2, h2.reshape(n, d), e1, e2, tau, peer_u.astype(BF16),
                      peer_v.T.astype(BF16), block_tokens=512, rows_per_step=8)
    return out.reshape(b, s, d)


def kernel(x, mem, positions, norm_mix_g, w_in, qnorm_g_dil, knorm_g_dil, out_norm_g_dil, out_norm_g_sb, w_out, norm_mem_g, norm_memtok_g, w_q_mem, w_kv_mem, qnorm_g_mem, knorm_g_mem, w_o_mem, norm_ffn_g, w_q_peer, peer_sub_keys, peer_u, peer_v):
    h = x
    for l in range(norm_mix_g.shape[0]):
        h = _layer(h, mem, positions, norm_mix_g[l], w_in[l], qnorm_g_dil[l], knorm_g_dil[l],
                   out_norm_g_dil[l], out_norm_g_sb[l], w_out[l], norm_mem_g[l], norm_memtok_g[l],
                   w_q_mem[l], w_kv_mem[l], qnorm_g_mem[l], knorm_g_mem[l], w_o_mem[l],
                   norm_ffn_g[l], w_q_peer[l], peer_sub_keys[l], peer_u[l], peer_v[l])
    return h
```
